```python
import jax, jax.numpy as jnp
from jax import lax
import numpy as np

D_MODEL = 1024
BATCH = 2
SEQ = 8192
DEPTH = 1

D_MIX = D_MODEL
ATT_HEADS = 8
ATT_KV_HEADS = 2
ATT_HEAD_DIM = 64
WINDOW = 128
ATT_BLOCK = 128
GLA_HEADS = 4
GLA_DK = 64
GLA_DV = 128
GLA_GATE_RANK = 16
GLA_TAU = 16.0
GLA_CHUNK = 64
PEER_HEADS = 8
PEER_QDIM = 256
N_KEYS = 128
N_EXPERTS = N_KEYS * N_KEYS
PEER_TOPK = 16
PEER_BLOCK = 128
NORM_EPS = 1e-6

ATT_Q_W = ATT_HEADS * ATT_HEAD_DIM
ATT_KV_W = ATT_KV_HEADS * ATT_HEAD_DIM
GLA_QK_W = GLA_HEADS * GLA_DK
GLA_V_W = GLA_HEADS * GLA_DV
IN_SIZES = (ATT_Q_W, ATT_KV_W, ATT_KV_W, GLA_QK_W, GLA_QK_W, GLA_V_W, GLA_V_W, GLA_GATE_RANK)
IN_WIDTH = sum(IN_SIZES)

kernel_name = "hymba_swa_sink_gla_peer_adaln"


def rms_norm(x, w):
    xf = x.astype(jnp.float32)
    y = xf * lax.rsqrt(jnp.mean(xf * xf, axis=-1, keepdims=True) + NORM_EPS)
    return (y * w.astype(jnp.float32)).astype(x.dtype)


def sliding_window_attention(q, k, v, sinks):
    B, S = q.shape[0], q.shape[1]
    nb = S // ATT_BLOCK
    G = ATT_HEADS // ATT_KV_HEADS
    qb = q.reshape(B, nb, ATT_BLOCK, ATT_KV_HEADS, G, ATT_HEAD_DIM)

    def band(t):
        tb = t.reshape(B, nb, ATT_BLOCK, ATT_KV_HEADS, ATT_HEAD_DIM)
        prev = jnp.pad(tb, ((0, 0), (1, 0), (0, 0), (0, 0), (0, 0)))[:, :-1]
        return jnp.concatenate([prev, tb], axis=2)

    kb, vb = band(k), band(v)
    scores = jnp.einsum('bnqhgd,bnkhd->bnhgqk', qb, kb).astype(jnp.float32) * (ATT_HEAD_DIM ** -0.5)
    blk = jnp.arange(nb)[:, None, None]
    qpos = blk * ATT_BLOCK + jnp.arange(ATT_BLOCK)[None, :, None]
    kpos = (blk - 1) * ATT_BLOCK + jnp.arange(2 * ATT_BLOCK)[None, None, :]
    rel = qpos - kpos
    mask = (rel >= 0) & (rel < WINDOW) & (kpos >= 0)
    scores = jnp.where(mask[None, :, None, None], scores, -jnp.inf)
    sink = sinks.astype(jnp.float32).reshape(ATT_KV_HEADS, G)[None, None, :, :, None, None]
    sink = jnp.broadcast_to(sink, scores.shape[:-1] + (1,))
    probs = jax.nn.softmax(jnp.concatenate([scores, sink], axis=-1), axis=-1)[..., :-1]
    out = jnp.einsum('bnhgqk,bnkhd->bnqhgd', probs.astype(v.dtype), vb)
    return out.reshape(B, S, ATT_Q_W)


def gla_chunked(q, k, v, log_a):
    B, S = q.shape[0], q.shape[1]
    nc = S // GLA_CHUNK

    def chunks(t):
        return t.astype(jnp.float32).reshape(B, nc, GLA_CHUNK, GLA_HEADS, t.shape[-1]).transpose(1, 0, 3, 2, 4)

    qc = chunks(q) * (GLA_DK ** -0.5)
    kc, vc, gc = chunks(k), chunks(v), chunks(log_a)
    causal = jnp.tril(jnp.ones((GLA_CHUNK, GLA_CHUNK), dtype=bool))

    def step(state, inp):
        qt, kt, vt, gt = inp
        b = jnp.cumsum(gt, axis=2)
        inter = jnp.einsum('bhtd,bhde->bhte', qt * jnp.exp(b), state)
        diff = b[:, :, :, None, :] - b[:, :, None, :, :]
        decay = jnp.exp(jnp.where(causal[:, :, None], diff, -jnp.inf))
        att = jnp.einsum('bhtd,bhsd,bhtsd->bhts', qt, kt, decay)
        intra = jnp.einsum('bhts,bhse->bhte', att, vt)
        b_last = b[:, :, -1:, :]
        new_state = jnp.exp(b_last[:, :, 0, :, None]) * state + jnp.einsum(
            'bhsd,bhse->bhde', kt * jnp.exp(b_last - b), vt)
        return new_state, inter + intra

    state0 = jnp.zeros((B, GLA_HEADS, GLA_DK, GLA_DV), jnp.float32)
    _, out = lax.scan(step, state0, (qc, kc, vc, gc))
    return out.transpose(1, 0, 3, 2, 4).reshape(B, S, GLA_HEADS, GLA_DV)


def peer_ffn(h, wq, subkeys, u, v):
    B, S, D = h.shape
    q = (h @ wq).reshape(B, S, PEER_HEADS, 2, PEER_QDIM // 2)
    sub_scores = jnp.einsum('bshpk,hpnk->bshpn', q, subkeys).astype(jnp.float32)
    top_v, top_i = lax.top_k(sub_scores, PEER_TOPK)
    cand = (top_v[..., 0, :, None] + top_v[..., 1, None, :]).reshape(B, S, PEER_HEADS, PEER_TOPK * PEER_TOPK)
    cand_i = (top_i[..., 0, :, None] * N_KEYS + top_i[..., 1, None, :]).reshape(B, S, PEER_HEADS, PEER_TOPK * PEER_TOPK)
    best_v, best_pos = lax.top_k(cand, PEER_TOPK)
    expert = jnp.take_along_axis(cand_i, best_pos, axis=-1)
    gates = jax.nn.softmax(best_v, axis=-1).astype(h.dtype)
    nblk = (B * S) // PEER_BLOCK
    xs = h.reshape(nblk, PEER_BLOCK, D)
    es = expert.reshape(nblk, PEER_BLOCK, PEER_HEADS * PEER_TOPK)
    gs = gates.reshape(nblk, PEER_BLOCK, PEER_HEADS * PEER_TOPK)

    def block(args):
        xb, eb, gb = args
        ub = jnp.take(u, eb, axis=0)
        hid = jax.nn.gelu(jnp.einsum('tkd,td->tk', ub, xb), approximate=False) * gb
        vb = jnp.take(v, eb, axis=0)
        return jnp.einsum('tk,tkd->td', hid, vb)

    return lax.map(block, (xs, es, gs)).reshape(B, S, D)


def setup_inputs(seed: int = 0) -> dict:
    key = jax.random.key(seed)
    ks = jax.random.split(key, 20)
    f32 = jnp.float32
    nrm = lambda k, shape, s: jax.random.normal(k, shape, f32) * s
    L = DEPTH
    return {
        "x": nrm(ks[0], (BATCH, SEQ, D_MODEL), 1.0),
        "c": nrm(ks[1], (BATCH, D_MODEL), 1.0),
        "w_ada": nrm(ks[2], (L, D_MODEL, 6 * D_MODEL), 0.5 * D_MODEL ** -0.5),
        "b_ada": nrm(ks[3], (L, 6 * D_MODEL), 0.02),
        "norm1_w": 1.0 + nrm(ks[4], (L, D_MODEL), 0.02),
        "w_in": nrm(ks[5], (L, D_MODEL, IN_WIDTH), D_MODEL ** -0.5),
        "attn_sinks": nrm(ks[6], (L, ATT_HEADS), 0.5),
        "gla_gate_up": nrm(ks[7], (L, GLA_GATE_RANK, GLA_QK_W), GLA_GATE_RANK ** -0.5),
        "gla_gate_bias": nrm(ks[8], (L, GLA_QK_W), 0.1),
        "gla_norm_w": 1.0 + nrm(ks[9], (L, GLA_DV), 0.02),
        "w_out": nrm(ks[10], (L, D_MIX, D_MODEL), D_MIX ** -0.5),
        "norm2_w": 1.0 + nrm(ks[11], (L, D_MODEL), 0.02),
        "peer_wq": nrm(ks[12], (L, D_MODEL, PEER_HEADS * PEER_QDIM), D_MODEL ** -0.5),
        "peer_subkeys": nrm(ks[13], (L, PEER_HEADS, 2, N_KEYS, PEER_QDIM // 2), (PEER_QDIM // 2) ** -0.5),
        "peer_u": nrm(ks[14], (L, N_EXPERTS, D_MODEL), D_MODEL ** -0.5),
        "peer_v": nrm(ks[15], (L, N_EXPERTS, D_MODEL), (PEER_HEADS * PEER_TOPK) ** -0.5),
        "final_norm_w": 1.0 + nrm(ks[16], (D_MODEL,), 0.02),
    }


def reference(x, c, w_ada, b_ada, norm1_w, w_in, attn_sinks, gla_gate_up, gla_gate_bias, gla_norm_w,
              w_out, norm2_w, peer_wq, peer_subkeys, peer_u, peer_v, final_norm_w):
    B, S, _ = x.shape
    split_points = np.cumsum(np.array(IN_SIZES))[:-1].tolist()
    for l in range(DEPTH):
        mod = jax.nn.silu(c) @ w_ada[l] + b_ada[l]
        shift1, scale1, gate1, shift2, scale2, gate2 = jnp.split(mod[:, None, :], 6, axis=-1)

        h = rms_norm(x, norm1_w[l]) * (1.0 + scale1) + shift1
        proj = h @ w_in[l]
        aq, ak, av, gq, gk, gv, gg, glr = jnp.split(proj, split_points, axis=-1)
        att = sliding_window_attention(
            aq.reshape(B, S, ATT_HEADS, ATT_HEAD_DIM),
            ak.reshape(B, S, ATT_KV_HEADS, ATT_HEAD_DIM),
            av.reshape(B, S, ATT_KV_HEADS, ATT_HEAD_DIM),
            attn_sinks[l])
        log_a = jax.nn.log_sigmoid((glr @ gla_gate_up[l] + gla_gate_bias[l]).astype(jnp.float32)) / GLA_TAU
        go = gla_chunked(
            gq.reshape(B, S, GLA_HEADS, GLA_DK),
            gk.reshape(B, S, GLA_HEADS, GLA_DK),
            gv.reshape(B, S, GLA_HEADS, GLA_DV),
            log_a.reshape(B, S, GLA_HEADS, GLA_DK))
        go = rms_norm(go, gla_norm_w[l]).reshape(B, S, GLA_V_W).astype(x.dtype) * jax.nn.silu(gg)
        mixed = jnp.concatenate([att, go], axis=-1) @ w_out[l]
        x = x + gate1 * mixed

        h2 = rms_norm(x, norm2_w[l]) * (1.0 + scale2) + shift2
        x = x + gate2 * peer_ffn(h2, peer_wq[l], peer_subkeys[l], peer_u[l], peer_v[l])
    return rms_norm(x, final_norm_w)
```

```python
import functools

import jax
import jax.numpy as jnp
from jax import lax
from jax.experimental import pallas as pl
from jax.experimental.pallas import tpu as pltpu

F32 = jnp.float32
BF16 = jnp.bfloat16
HIGHEST = lax.Precision.HIGHEST

LANES = 128
SUBLANES = 8

ATT_HEADS = 8
ATT_KV_HEADS = 2
ATT_HEAD_DIM = 64
WINDOW = 128
GLA_HEADS = 4
GLA_DK = 64
GLA_DV = 128
GLA_GATE_RANK = 16
GLA_TAU = 16.0
GLA_CHUNK = 64
PEER_HEADS = 8
PEER_QDIM = 256
N_KEYS = 128
PEER_TOPK = 16
NORM_EPS = 1e-6

ATT_Q_W = ATT_HEADS * ATT_HEAD_DIM
ATT_KV_W = ATT_KV_HEADS * ATT_HEAD_DIM
GLA_QK_W = GLA_HEADS * GLA_DK
GLA_V_W = GLA_HEADS * GLA_DV
PEER_SLOTS = PEER_HEADS * PEER_TOPK

COEF_PITCH = N_KEYS + SUBLANES

NEG_INF = float("-inf")


def _cparams(sem, vmem_mib):
    return pltpu.CompilerParams(dimension_semantics=sem, vmem_limit_bytes=vmem_mib * 1024 * 1024)


def _rms(x, w):
    return x * lax.rsqrt(jnp.mean(x * x, axis=-1, keepdims=True) + NORM_EPS) * w


def _silu(x):
    return x / (1.0 + jnp.exp(-x))


def _mod_kernel(c_ref, w_ref, b_ref, o_ref):
    o_ref[...] = jnp.dot(_silu(c_ref[...]), w_ref[...], preferred_element_type=F32, precision=HIGHEST) + b_ref[...]


def _modulation(c, w_ada, b_ada):
    bsz, d = c.shape
    n = w_ada.shape[1]
    rows = -(-bsz // SUBLANES) * SUBLANES
    cp = jnp.zeros((rows, d), F32).at[:bsz].set(c)
    bn = d
    out = pl.pallas_call(
        _mod_kernel,
        grid=(n // bn,),
        in_specs=[pl.BlockSpec((rows, d), lambda j: (0, 0)),
                  pl.BlockSpec((d, bn), lambda j: (0, j)),
                  pl.BlockSpec((1, bn), lambda j: (0, j))],
        out_specs=pl.BlockSpec((rows, bn), lambda j: (0, j)),
        out_shape=jax.ShapeDtypeStruct((rows, n), F32),
        compiler_params=_cparams(("parallel",), 24),
        name="adaln_mod",
    )(cp, w_ada, b_ada.reshape(1, n))
    mod = out[:bsz].reshape(bsz, n // d, d)
    return jnp.concatenate([mod, jnp.zeros((bsz, SUBLANES - n // d, d), F32)], axis=1)


_PROJ_COLS = (
    ("aq", ATT_Q_W, BF16), ("ak", ATT_KV_W, BF16), ("akx", ATT_KV_W, BF16), ("av", ATT_KV_W, BF16),
    ("avx", ATT_KV_W, BF16), ("gq", GLA_QK_W, F32), ("gk", GLA_QK_W, F32), ("gv", GLA_V_W, BF16),
    ("gg", GLA_V_W, F32),
)


def _proj_kernel(x_ref, mod_ref, nw_ref, w_ref, gu_ref, gb_ref, *out_refs):
    x = x_ref[...]
    h = _rms(x, nw_ref[...]) * (1.0 + mod_ref[0, 1:2, :]) + mod_ref[0, 0:1, :]
    p = jnp.dot(h.astype(BF16), w_ref[...], preferred_element_type=F32)
    off = 0
    for (_, width, dt), o_ref in zip(_PROJ_COLS, out_refs[:-1]):
        o_ref[...] = p[:, off:off + width].astype(dt)
        off += width
    glr = p[:, off:off + LANES]
    z = jnp.dot(glr, gu_ref[...], preferred_element_type=F32, precision=HIGHEST) + gb_ref[...]
    log_sig = jnp.minimum(z, 0.0) - jnp.log(1.0 + jnp.exp(-jnp.abs(z)))
    out_refs[-1][...] = log_sig * (1.0 / GLA_TAU)


def _in_proj(x2, mod, norm_w, w_in, gate_up, gate_bias, seq, tb):
    t, d = x2.shape
    sizes = (ATT_Q_W, ATT_KV_W, ATT_KV_W, GLA_QK_W, GLA_QK_W, GLA_V_W, GLA_V_W, GLA_GATE_RANK)
    offs = [0]
    for s in sizes:
        offs.append(offs[-1] + s)
    aq, ak, av, gq, gk, gv, gg, glr = (w_in[:, offs[i]:offs[i + 1]] for i in range(8))
    half = ATT_HEAD_DIM
    swap = lambda w: jnp.concatenate([w[:, half:], w[:, :half]], axis=1)
    glr_p = jnp.pad(glr, ((0, 0), (0, LANES - GLA_GATE_RANK)))
    w_all = jnp.concatenate([aq, ak, swap(ak), av, swap(av), gq, gk, gv, gg, glr_p], axis=1).astype(BF16)
    gu_p = jnp.pad(gate_up, ((0, LANES - GLA_GATE_RANK), (0, 0)))
    ncols = w_all.shape[1]
    steps_per_seq = seq // tb
    row = lambda i: (i, 0)
    fixed = lambda i: (0, 0)
    out_shapes = [jax.ShapeDtypeStruct((t, w), dt) for _, w, dt in _PROJ_COLS]
    out_shapes.append(jax.ShapeDtypeStruct((t, GLA_QK_W), F32))
    out_specs = [pl.BlockSpec((tb, w), row) for _, w, _ in _PROJ_COLS] + [pl.BlockSpec((tb, GLA_QK_W), row)]
    return pl.pallas_call(
        _proj_kernel,
        grid=(t // tb,),
        in_specs=[pl.BlockSpec((tb, d), row),
                  pl.BlockSpec((1, SUBLANES, d), lambda i: (i // steps_per_seq, 0, 0)),
                  pl.BlockSpec((1, d), fixed),
                  pl.BlockSpec((d, ncols), fixed),
                  pl.BlockSpec((LANES, GLA_QK_W), fixed),
                  pl.BlockSpec((1, GLA_QK_W), fixed)],
        out_specs=out_specs,
        out_shape=out_shapes,
        compiler_params=_cparams(("parallel",), 48),
        name="norm1_in_proj",
    )(x2, mod, norm_w.reshape(1, d), w_all, gu_p, gate_bias.reshape(1, GLA_QK_W))


def _attn_kernel(sink_ref, q_ref, k_ref, kx_ref, v_ref, vx_ref, kp_ref, kxp_ref, vp_ref, vxp_ref, o_ref,
                 *, steps_per_seq, nsub):
    first = (pl.program_id(0) % steps_per_seq) == 0
    blk = WINDOW
    lane = lax.broadcasted_iota(jnp.int32, (2 * blk, LANES), 1)
    lo = lane < ATT_HEAD_DIM
    r = lax.broadcasted_iota(jnp.int32, (blk, 2 * blk), 0)
    c = lax.broadcasted_iota(jnp.int32, (blk, 2 * blk), 1)
    band = (c > r) & (c <= r + blk)
    scale = ATT_HEAD_DIM ** -0.5
    pairs = ATT_HEADS // 2
    pairs_per_kv = pairs // ATT_KV_HEADS
    zero = jnp.zeros((), BF16)
    for j in range(nsub):
        rows = slice(j * blk, (j + 1) * blk)
        if j == 0:
            cat = lambda p_ref, c_ref: jnp.concatenate([p_ref[...], c_ref[0:blk, :]], axis=0)
            kb, kxb, vb, vxb = cat(kp_ref, k_ref), cat(kxp_ref, kx_ref), cat(vp_ref, v_ref), cat(vxp_ref, vx_ref)
            mask = band & jnp.logical_not(first & (c < blk))
        else:
            prev = slice((j - 1) * blk, (j + 1) * blk)
            kb, kxb, vb, vxb = k_ref[prev, :], kx_ref[prev, :], v_ref[prev, :], vx_ref[prev, :]
            mask = band
        for p in range(pairs):
            kv = p // pairs_per_kv
            qp = q_ref[rows, p * LANES:(p + 1) * LANES]
            k_lo = jnp.where(lo, kb if kv == 0 else kxb, zero)
            k_hi = jnp.where(lo, zero, kxb if kv == 0 else kb)
            v_lo = jnp.where(lo, vb if kv == 0 else vxb, zero)
            v_hi = jnp.where(lo, zero, vxb if kv == 0 else vb)
            out = None
            for hh, (kk, vv) in enumerate(((k_lo, v_lo), (k_hi, v_hi))):
                sink = sink_ref[2 * p + hh]
                s = lax.dot_general(qp, kk, (((1,), (1,)), ((), ())), preferred_element_type=F32) * scale
                s = jnp.where(mask, s, NEG_INF)
                m = jnp.maximum(jnp.max(s, axis=-1, keepdims=True), sink)
                e = jnp.exp(s - m)
                denom = jnp.sum(e, axis=-1, keepdims=True) + jnp.exp(sink - m)
                pv = jnp.dot(e.astype(BF16), vv, preferred_element_type=F32)
                contrib = pv * (1.0 / denom)
                out = contrib if out is None else out + contrib
            o_ref[rows, p * LANES:(p + 1) * LANES] = out.astype(o_ref.dtype)


def _attention(aq, ak, akx, av, avx, sinks, seq, tq):
    t = aq.shape[0]
    nsub = tq // WINDOW
    steps_per_seq = seq // tq
    row = lambda i: (i, 0)
    prev = lambda i: (jnp.maximum(i * nsub - 1, 0), 0)
    kv_spec = pl.BlockSpec((tq, ATT_KV_W), row)
    kv_prev = pl.BlockSpec((WINDOW, ATT_KV_W), prev)
    return pl.pallas_call(
        functools.partial(_attn_kernel, steps_per_seq=steps_per_seq, nsub=nsub),
        grid=(t // tq,),
        in_specs=[pl.BlockSpec(memory_space=pltpu.SMEM),
                  pl.BlockSpec((tq, ATT_Q_W), row), kv_spec, kv_spec, kv_spec, kv_spec,
                  kv_prev, kv_prev, kv_prev, kv_prev],
        out_specs=pl.BlockSpec((tq, ATT_Q_W), row),
        out_shape=jax.ShapeDtypeStruct((t, ATT_Q_W), BF16),
        compiler_params=_cparams(("parallel",), 32),
        name="window_attention",
    )(sinks, aq, ak, akx, av, avx, ak, akx, av, avx)


def _gla_kernel(q_ref, k_ref, la_ref, v_ref, g_ref, nw_ref, o_ref, state_ref, *, nchunk):
    @pl.when(pl.program_id(1) == 0)
    def _():
        state_ref[...] = jnp.zeros_like(state_ref)

    ch = GLA_CHUNK
    tri = (lax.broadcasted_iota(jnp.int32, (ch, ch), 0) >= lax.broadcasted_iota(jnp.int32, (ch, ch), 1))
    tri_f = tri.astype(F32)
    ones = jnp.ones((ch, LANES), F32)
    lane = lax.broadcasted_iota(jnp.int32, (ch, LANES), 1)
    srow = lax.broadcasted_iota(jnp.int32, (LANES, GLA_DV), 0)
    scale = GLA_DK ** -0.5
    mid = ch // 2 - 1
    nw = nw_ref[...]
    tn = (((0,), (0,)), ((), ()))
    nt = (((1,), (1,)), ((), ()))
    for cidx in range(nchunk):
        rows = slice(cidx * ch, (cidx + 1) * ch)
        for pair in range(GLA_HEADS // 2):
            cols = slice(pair * LANES, (pair + 1) * LANES)
            g = la_ref[rows, cols]
            b = jnp.dot(tri_f, g, preferred_element_type=F32, precision=HIGHEST)
            b_last_col = lax.dot_general(g, ones, tn, preferred_element_type=F32, precision=HIGHEST)
            decay = jnp.exp(b_last_col)
            b_mid = b[mid:mid + 1, :]
            b_last = b[ch - 1:ch, :]
            qt = q_ref[rows, cols] * scale * jnp.exp(b - b_mid)
            kt = k_ref[rows, cols] * jnp.exp(b_mid - b)
            q_in = (qt * jnp.exp(b_mid)).astype(BF16)
            k_st = (kt * jnp.exp(b_last - b_mid)).astype(BF16)
            qt = qt.astype(BF16)
            for hh in range(2):
                h = 2 * pair + hh
                mine = (lane >= hh * GLA_DK) & (lane < (hh + 1) * GLA_DK)
                ktm = jnp.where(mine, kt, 0.0).astype(BF16)
                att = lax.dot_general(qt, ktm, nt, preferred_element_type=F32)
                att = jnp.where(tri, att, 0.0)
                vh = v_ref[rows, h * GLA_DV:(h + 1) * GLA_DV]
                st = state_ref[h]
                o = (jnp.dot(att.astype(BF16), vh, preferred_element_type=F32)
                     + jnp.dot(q_in, st.astype(BF16), preferred_element_type=F32))
                gate = _silu(g_ref[rows, h * GLA_DV:(h + 1) * GLA_DV])
                o_ref[rows, h * GLA_DV:(h + 1) * GLA_DV] = (_rms(o, nw) * gate).astype(o_ref.dtype)
                upd = lax.dot_general(k_st, vh, tn, preferred_element_type=F32)
                rmask = (srow >= hh * GLA_DK) & (srow < (hh + 1) * GLA_DK)
                state_ref[h] = jnp.where(rmask, decay * st + upd, 0.0)


def _gla(gq, gk, la, gv, gg, norm_w, bsz, seq, cb):
    t = gq.shape[0]
    steps = seq // cb
    row = lambda b, s: (b * steps + s, 0)
    return pl.pallas_call(
        functools.partial(_gla_kernel, nchunk=cb // GLA_CHUNK),
        grid=(bsz, steps),
        in_specs=[pl.BlockSpec((cb, GLA_QK_W), row), pl.BlockSpec((cb, GLA_QK_W), row),
                  pl.BlockSpec((cb, GLA_QK_W), row), pl.BlockSpec((cb, GLA_V_W), row),
                  pl.BlockSpec((cb, GLA_V_W), row), pl.BlockSpec((1, GLA_DV), lambda b, s: (0, 0))],
        out_specs=pl.BlockSpec((cb, GLA_V_W), row),
        out_shape=jax.ShapeDtypeStruct((t, GLA_V_W), BF16),
        scratch_shapes=[pltpu.VMEM((GLA_HEADS, LANES, GLA_DV), F32)],
        compiler_params=_cparams(("parallel", "arbitrary"), 32),
        name="gated_linear_attention",
    )(gq, gk, la, gv, gg, norm_w.reshape(1, GLA_DV))


def _mix_out_kernel(att_ref, go_ref, x_ref, mod_ref, wa_ref, wg_ref, nw_ref, wq_ref, x1_ref, h2_ref, qp_ref):
    mixed = (jnp.dot(att_ref[...], wa_ref[...], preferred_element_type=F32)
             + jnp.dot(go_ref[...], wg_ref[...], preferred_element_type=F32))
    x1 = x_ref[...] + mod_ref[0, 2:3, :] * mixed
    x1_ref[...] = x1
    h2 = (_rms(x1, nw_ref[...]) * (1.0 + mod_ref[0, 4:5, :]) + mod_ref[0, 3:4, :]).astype(BF16)
    h2_ref[...] = h2
    qp_ref[...] = jnp.dot(h2, wq_ref[...], preferred_element_type=F32).astype(BF16)


def _mix_out(att, go, x2, mod, w_out, norm_w, wq, seq, tb):
    t, d = x2.shape
    nq = wq.shape[1]
    steps_per_seq = seq // tb
    row = lambda i: (i, 0)
    fixed = lambda i: (0, 0)
    wa = w_out[:ATT_Q_W].astype(BF16)
    wg = w_out[ATT_Q_W:].astype(BF16)
    return pl.pallas_call(
        _mix_out_kernel,
        grid=(t // tb,),
        in_specs=[pl.BlockSpec((tb, ATT_Q_W), row), pl.BlockSpec((tb, GLA_V_W), row), pl.BlockSpec((tb, d), row),
                  pl.BlockSpec((1, SUBLANES, d), lambda i: (i // steps_per_seq, 0, 0)),
                  pl.BlockSpec((ATT_Q_W, d), fixed), pl.BlockSpec((GLA_V_W, d), fixed),
                  pl.BlockSpec((1, d), fixed), pl.BlockSpec((d, nq), fixed)],
        out_specs=[pl.BlockSpec((tb, d), row), pl.BlockSpec((tb, d), row), pl.BlockSpec((tb, nq), row)],
        out_shape=[jax.ShapeDtypeStruct((t, d), F32), jax.ShapeDtypeStruct((t, d), BF16),
                   jax.ShapeDtypeStruct((t, nq), BF16)],
        compiler_params=_cparams(("parallel",), 48),
        name="out_proj_norm2_peer_query",
    )(att, go, x2, mod, wa, wg, norm_w.reshape(1, d), wq.astype(BF16))


def _top16(x):
    iota = lax.broadcasted_iota(jnp.int32, x.shape, 0).astype(F32)
    big = float(x.shape[0])
    vals, idxs = [], []
    for _ in range(PEER_TOPK):
        m = jnp.max(x, axis=0, keepdims=True)
        idx = jnp.min(jnp.where(x == m, iota, big), axis=0, keepdims=True)
        vals.append(m)
        idxs.append(idx)
        x = jnp.where(iota == idx, NEG_INF, x)
    return jnp.concatenate(vals, axis=0), jnp.concatenate(idxs, axis=0)


def _pick_rows(table, sel):
    out = jnp.zeros_like(table)
    for r in range(table.shape[0]):
        out = out + jnp.where(sel == float(r), table[r:r + 1, :], 0.0)
    return out


def _topk_kernel(q_ref, sk_ref, i_ref, j_ref, g_ref):
    q = q_ref[...]
    nt = (((1,), (1,)), ((), ()))
    tops = []
    for p in range(2):
        s = lax.dot_general(sk_ref[0, p], q[:, p * N_KEYS:(p + 1) * N_KEYS], nt, preferred_element_type=F32)
        tops.append(_top16(s))
    (a, ia), (b, ib) = tops
    cand = jnp.concatenate([a[r:r + 1, :] + b for r in range(PEER_TOPK)], axis=0)
    best, pos = _top16(cand)
    pa = jnp.floor(pos * (1.0 / PEER_TOPK))
    pb = pos - pa * PEER_TOPK
    i_ref[...] = _pick_rows(ia, pa).astype(jnp.int32)
    j_ref[...] = _pick_rows(ib, pb).astype(jnp.int32)
    e = jnp.exp(best - best[0:1, :])
    g_ref[...] = e / jnp.sum(e, axis=0, keepdims=True)


def _peer_topk(qp, subkeys, tb):
    t = qp.shape[0]
    sk = subkeys.astype(BF16)
    blk = lambda i, h: (h, i)
    outs = pl.pallas_call(
        _topk_kernel,
        grid=(t // tb, PEER_HEADS),
        in_specs=[pl.BlockSpec((tb, PEER_QDIM), lambda i, h: (i, h)),
                  pl.BlockSpec((1, 2, N_KEYS, PEER_QDIM // 2), lambda i, h: (h, 0, 0, 0))],
        out_specs=[pl.BlockSpec((PEER_TOPK, tb), blk)] * 3,
        out_shape=[jax.ShapeDtypeStruct((PEER_SLOTS, t), jnp.int32), jax.ShapeDtypeStruct((PEER_SLOTS, t), jnp.int32),
                   jax.ShapeDtypeStruct((PEER_SLOTS, t), F32)],
        compiler_params=_cparams(("parallel", "parallel"), 32),
        name="peer_product_key_topk",
    )(qp, sk)
    return tuple(o.T for o in outs)


def _gelu(x):
    return 0.5 * x * (1.0 + lax.erf(x * (2.0 ** -0.5)))


def _peer_score_kernel(h_ref, ut_ref, i_ref, j_ref, g_ref, c_ref, acc_ref, *, groups):
    e = pl.program_id(1)

    @pl.when(e == 0)
    def _():
        acc_ref[...] = jnp.zeros_like(acc_ref)

    s = jnp.dot(h_ref[...], ut_ref[...], preferred_element_type=F32)
    ii = i_ref[...]
    jj = j_ref[...]
    acc = acc_ref[...]
    for g in range(groups):
        picked = jnp.take_along_axis(s[:, g * N_KEYS:(g + 1) * N_KEYS], jj, axis=1)
        acc = acc + jnp.where(ii == e * groups + g, picked, 0.0)
    acc_ref[...] = acc

    @pl.when(e == pl.num_programs(1) - 1)
    def _():
        c_ref[...] = _gelu(acc) * g_ref[...]


def _peer_scores(h2, ut, ii, jj, gates, tb, te):
    t, d = h2.shape
    ne = ut.shape[1]
    row = lambda i, e: (i, 0)
    slot_spec = pl.BlockSpec((tb, PEER_SLOTS), row)
    return pl.pallas_call(
        functools.partial(_peer_score_kernel, groups=te // N_KEYS),
        grid=(t // tb, ne // te),
        in_specs=[pl.BlockSpec((tb, d), row), pl.BlockSpec((d, te), lambda i, e: (0, e)),
                  slot_spec, slot_spec, slot_spec],
        out_specs=slot_spec,
        out_shape=jax.ShapeDtypeStruct((t, PEER_SLOTS), F32),
        scratch_shapes=[pltpu.VMEM((tb, PEER_SLOTS), F32)],
        compiler_params=_cparams(("parallel", "arbitrary"), 48),
        name="peer_expert_scores",
    )(h2, ut, ii, jj, gates)


def _peer_mix_kernel(c_ref, i_ref, j_ref, v_ref, x1_ref, mod_ref, fw_ref, o_ref, coef_ref, acc_ref, *, tb, groups):
    e = pl.program_id(1)

    @pl.when(e == 0)
    def _():
        acc_ref[...] = jnp.zeros_like(acc_ref)
        sub = lax.broadcasted_iota(jnp.int32, (N_KEYS, PEER_SLOTS), 0)
        nt = (((1,), (1,)), ((), ()))

        def one_token(t, carry):
            irow = i_ref[pl.ds(t, 1), :]
            jrow = j_ref[pl.ds(t, 1), :]
            crow = c_ref[pl.ds(t, 1), :]
            a_t = jnp.where(sub == irow, crow, 0.0).astype(BF16)
            b_t = jnp.where(sub == jrow, 1.0, 0.0).astype(BF16)
            tile = lax.dot_general(a_t, b_t, nt, preferred_element_type=F32)
            coef_ref[pl.ds(pl.multiple_of(t * COEF_PITCH, SUBLANES), N_KEYS), :] = tile
            return carry

        lax.fori_loop(0, tb, one_token, 0)

    lhs = jnp.concatenate(
        [coef_ref[pl.ds(e * groups + g, tb, stride=COEF_PITCH), :].astype(BF16) for g in range(groups)], axis=1)
    acc_ref[...] += jnp.dot(lhs, v_ref[...], preferred_element_type=F32)

    @pl.when(e == pl.num_programs(1) - 1)
    def _():
        y = x1_ref[...] + mod_ref[0, 5:6, :] * acc_ref[...]
        o_ref[...] = _rms(y, fw_ref[...])


def _peer_mix(c, ii, jj, v_bf, x1, mod, final_w, seq, tb, te):
    t, d = x1.shape
    ne = v_bf.shape[0]
    steps_per_seq = seq // tb
    row = lambda i, e: (i, 0)
    slot_spec = pl.BlockSpec((tb, PEER_SLOTS), row)
    return pl.pallas_call(
        functools.partial(_peer_mix_kernel, tb=tb, groups=te // N_KEYS),
        grid=(t // tb, ne // te),
        in_specs=[slot_spec, slot_spec, slot_spec, pl.BlockSpec((te, d), lambda i, e: (e, 0)),
                  pl.BlockSpec((tb, d), row),
                  pl.BlockSpec((1, SUBLANES, d), lambda i, e: (i // steps_per_seq, 0, 0)),
                  pl.BlockSpec((1, d), lambda i, e: (0, 0))],
        out_specs=pl.BlockSpec((tb, d), row),
        out_shape=jax.ShapeDtypeStruct((t, d), F32),
        scratch_shapes=[pltpu.VMEM((tb * COEF_PITCH, N_KEYS), F32), pltpu.VMEM((tb, d), F32)],
        compiler_params=_cparams(("parallel", "arbitrary"), 56),
        name="peer_expert_mix_final_norm",
    )(c, ii, jj, v_bf, x1, mod, final_w.reshape(1, d))


def _pick_block(n, want):
    b = min(n, want)
    assert n % b == 0, (n, want)
    return b


def kernel(x, c, w_ada, b_ada, norm1_w, w_in, attn_sinks, gla_gate_up, gla_gate_bias, gla_norm_w, w_out, norm2_w,
           peer_wq, peer_subkeys, peer_u, peer_v, final_norm_w):
    bsz, seq, d = x.shape
    assert w_ada.shape[0] == 1, "the block is implemented for depth 1"
    assert seq % WINDOW == 0 and seq % GLA_CHUNK == 0
    assert peer_u.shape[1] == N_KEYS * N_KEYS
    tok_blk = _pick_block(seq, 512)
    x2 = x.reshape(bsz * seq, d)
    mod = _modulation(c, w_ada[0], b_ada[0])
    aq, ak, akx, av, avx, gq, gk, gv, gg, la = _in_proj(
        x2, mod, norm1_w[0], w_in[0], gla_gate_up[0], gla_gate_bias[0], seq, tok_blk)
    att = _attention(aq, ak, akx, av, avx, attn_sinks[0], seq, tok_blk)
    go = _gla(gq, gk, la, gv, gg, gla_norm_w[0], bsz, seq, tok_blk)
    x1, h2, qp = _mix_out(att, go, x2, mod, w_out[0], norm2_w[0], peer_wq[0], seq, tok_blk)
    ii, jj, gates = _peer_topk(qp, peer_subkeys[0], _pick_block(seq, 256))
    coef = _peer_scores(h2, peer_u[0].T.astype(BF16), ii, jj, gates, tok_blk, 2048)
    out = _peer_mix(coef, ii, jj, peer_v[0].astype(BF16), x1, mod, final_norm_w, seq, _pick_block(seq, 256), 2048)
    return out.reshape(bsz, seq, d)
```

```python
import functools

import jax
import jax.numpy as jnp
from jax import lax
from jax.experimental import pallas as pl
from jax.experimental.pallas import tpu as pltpu

F32 = jnp.float32
BF16 = jnp.bfloat16
HIGHEST = lax.Precision.HIGHEST

LANES = 128
SUBLANES = 8

ATT_HEADS = 8
ATT_KV_HEADS = 2
ATT_HEAD_DIM = 64
WINDOW = 128
GLA_HEADS = 4
GLA_DK = 64
GLA_DV = 128
GLA_GATE_RANK = 16
GLA_TAU = 16.0
GLA_CHUNK = 64
PEER_HEADS = 8
PEER_QDIM = 256
N_KEYS = 128
PEER_TOPK = 16
NORM_EPS = 1e-6

ATT_Q_W = ATT_HEADS * ATT_HEAD_DIM
ATT_KV_W = ATT_KV_HEADS * ATT_HEAD_DIM
GLA_QK_W = GLA_HEADS * GLA_DK
GLA_V_W = GLA_HEADS * GLA_DV
PEER_SLOTS = PEER_HEADS * PEER_TOPK

COEF_PITCH = N_KEYS + SUBLANES

NEG_INF = float("-inf")


def _cparams(sem, vmem_mib):
    return pltpu.CompilerParams(dimension_semantics=sem, vmem_limit_bytes=vmem_mib * 1024 * 1024)


def _rms(x, w):
    return x * lax.rsqrt(jnp.mean(x * x, axis=-1, keepdims=True) + NORM_EPS) * w


def _silu(x):
    return x / (1.0 + jnp.exp(-x))


def _mod_kernel(c_ref, w_ref, b_ref, o_ref):
    o_ref[...] = jnp.dot(_silu(c_ref[...]), w_ref[...], preferred_element_type=F32, precision=HIGHEST) + b_ref[...]


def _modulation(c, w_ada, b_ada):
    bsz, d = c.shape
    n = w_ada.shape[1]
    rows = -(-bsz // SUBLANES) * SUBLANES
    cp = jnp.zeros((rows, d), F32).at[:bsz].set(c)
    bn = d
    out = pl.pallas_call(
        _mod_kernel,
        grid=(n // bn,),
        in_specs=[pl.BlockSpec((rows, d), lambda j: (0, 0)),
                  pl.BlockSpec((d, bn), lambda j: (0, j)),
                  pl.BlockSpec((1, bn), lambda j: (0, j))],
        out_specs=pl.BlockSpec((rows, bn), lambda j: (0, j)),
        out_shape=jax.ShapeDtypeStruct((rows, n), F32),
        compiler_params=_cparams(("parallel",), 24),
        name="adaln_mod",
    )(cp, w_ada, b_ada.reshape(1, n))
    mod = out[:bsz].reshape(bsz, n // d, d)
    return jnp.concatenate([mod, jnp.zeros((bsz, SUBLANES - n // d, d), F32)], axis=1)


_PROJ_COLS = (
    ("aq", ATT_Q_W, BF16), ("ak", ATT_KV_W, BF16), ("akx", ATT_KV_W, BF16), ("av", ATT_KV_W, BF16),
    ("avx", ATT_KV_W, BF16), ("gq", GLA_QK_W, F32), ("gk", GLA_QK_W, F32), ("gv", GLA_V_W, BF16),
    ("gg", GLA_V_W, F32),
)


def _proj_kernel(x_ref, mod_ref, nw_ref, w_ref, gu_ref, gb_ref, *out_refs):
    x = x_ref[...]
    h = _rms(x, nw_ref[...]) * (1.0 + mod_ref[0, 1:2, :]) + mod_ref[0, 0:1, :]
    p = jnp.dot(h.astype(BF16), w_ref[...], preferred_element_type=F32)
    off = 0
    for (_, width, dt), o_ref in zip(_PROJ_COLS, out_refs[:-1]):
        o_ref[...] = p[:, off:off + width].astype(dt)
        off += width
    glr = p[:, off:off + LANES]
    z = jnp.dot(glr, gu_ref[...], preferred_element_type=F32, precision=HIGHEST) + gb_ref[...]
    log_sig = jnp.minimum(z, 0.0) - jnp.log(1.0 + jnp.exp(-jnp.abs(z)))
    out_refs[-1][...] = log_sig * (1.0 / GLA_TAU)


def _in_proj(x2, mod, norm_w, w_in, gate_up, gate_bias, seq, tb):
    t, d = x2.shape
    sizes = (ATT_Q_W, ATT_KV_W, ATT_KV_W, GLA_QK_W, GLA_QK_W, GLA_V_W, GLA_V_W, GLA_GATE_RANK)
    offs = [0]
    for s in sizes:
        offs.append(offs[-1] + s)
    aq, ak, av, gq, gk, gv, gg, glr = (w_in[:, offs[i]:offs[i + 1]] for i in range(8))
    half = ATT_HEAD_DIM
    swap = lambda w: jnp.concatenate([w[:, half:], w[:, :half]], axis=1)
    glr_p = jnp.pad(glr, ((0, 0), (0, LANES - GLA_GATE_RANK)))
    w_all = jnp.concatenate([aq, ak, swap(ak), av, swap(av), gq, gk, gv, gg, glr_p], axis=1).astype(BF16)
    gu_p = jnp.pad(gate_up, ((0, LANES - GLA_GATE_RANK), (0, 0)))
    ncols = w_all.shape[1]
    steps_per_seq = seq // tb
    row = lambda i: (i, 0)
    fixed = lambda i: (0, 0)
    out_shapes = [jax.ShapeDtypeStruct((t, w), dt) for _, w, dt in _PROJ_COLS]
    out_shapes.append(jax.ShapeDtypeStruct((t, GLA_QK_W), F32))
    out_specs = [pl.BlockSpec((tb, w), row) for _, w, _ in _PROJ_COLS] + [pl.BlockSpec((tb, GLA_QK_W), row)]
    return pl.pallas_call(
        _proj_kernel,
        grid=(t // tb,),
        in_specs=[pl.BlockSpec((tb, d), row),
                  pl.BlockSpec((1, SUBLANES, d), lambda i: (i // steps_per_seq, 0, 0)),
                  pl.BlockSpec((1, d), fixed),
                  pl.BlockSpec((d, ncols), fixed),
                  pl.BlockSpec((LANES, GLA_QK_W), fixed),
                  pl.BlockSpec((1, GLA_QK_W), fixed)],
        out_specs=out_specs,
        out_shape=out_shapes,
        compiler_params=_cparams(("parallel",), 48),
        name="norm1_in_proj",
    )(x2, mod, norm_w.reshape(1, d), w_all, gu_p, gate_bias.reshape(1, GLA_QK_W))


def _attn_kernel(sink_ref, q_ref, k_ref, kx_ref, v_ref, vx_ref, kp_ref, kxp_ref, vp_ref, vxp_ref, o_ref,
                 *, steps_per_seq, nsub):
    first = (pl.program_id(0) % steps_per_seq) == 0
    blk = WINDOW
    lane = lax.broadcasted_iota(jnp.int32, (2 * blk, LANES), 1)
    lo = lane < ATT_HEAD_DIM
    r = lax.broadcasted_iota(jnp.int32, (blk, 2 * blk), 0)
    c = lax.broadcasted_iota(jnp.int32, (blk, 2 * blk), 1)
    band = (c > r) & (c <= r + blk)
    scale = ATT_HEAD_DIM ** -0.5
    pairs = ATT_HEADS // 2
    pairs_per_kv = pairs // ATT_KV_HEADS
    zero = jnp.zeros((), BF16)
    for j in range(nsub):
        rows = slice(j * blk, (j + 1) * blk)
        if j == 0:
            cat = lambda p_ref, c_ref: jnp.concatenate([p_ref[...], c_ref[0:blk, :]], axis=0)
            kb, kxb, vb, vxb = cat(kp_ref, k_ref), cat(kxp_ref, kx_ref), cat(vp_ref, v_ref), cat(vxp_ref, vx_ref)
            mask = band & jnp.logical_not(first & (c < blk))
        else:
            prev = slice((j - 1) * blk, (j + 1) * blk)
            kb, kxb, vb, vxb = k_ref[prev, :], kx_ref[prev, :], v_ref[prev, :], vx_ref[prev, :]
            mask = band
        for p in range(pairs):
            kv = p // pairs_per_kv
            qp = q_ref[rows, p * LANES:(p + 1) * LANES]
            k_lo = jnp.where(lo, kb if kv == 0 else kxb, zero)
            k_hi = jnp.where(lo, zero, kxb if kv == 0 else kb)
            v_lo = jnp.where(lo, vb if kv == 0 else vxb, zero)
            v_hi = jnp.where(lo, zero, vxb if kv == 0 else vb)
            out = None
            for hh, (kk, vv) in enumerate(((k_lo, v_lo), (k_hi, v_hi))):
                sink = sink_ref[2 * p + hh]
                s = lax.dot_general(qp, kk, (((1,), (1,)), ((), ())), preferred_element_type=F32) * scale
                s = jnp.where(mask, s, NEG_INF)
                m = jnp.maximum(jnp.max(s, axis=-1, keepdims=True), sink)
                e = jnp.exp(s - m)
                denom = jnp.sum(e, axis=-1, keepdims=True) + jnp.exp(sink - m)
                pv = jnp.dot(e.astype(BF16), vv, preferred_element_type=F32)
                contrib = pv * (1.0 / denom)
                out = contrib if out is None else out + contrib
            o_ref[rows, p * LANES:(p + 1) * LANES] = out.astype(o_ref.dtype)


def _attention(aq, ak, akx, av, avx, sinks, seq, tq):
    t = aq.shape[0]
    nsub = tq // WINDOW
    steps_per_seq = seq // tq
    row = lambda i: (i, 0)
    prev = lambda i: (jnp.maximum(i * nsub - 1, 0), 0)
    kv_spec = pl.BlockSpec((tq, ATT_KV_W), row)
    kv_prev = pl.BlockSpec((WINDOW, ATT_KV_W), prev)
    return pl.pallas_call(
        functools.partial(_attn_kernel, steps_per_seq=steps_per_seq, nsub=nsub),
        grid=(t // tq,),
        in_specs=[pl.BlockSpec(memory_space=pltpu.SMEM),
                  pl.BlockSpec((tq, ATT_Q_W), row), kv_spec, kv_spec, kv_spec, kv_spec,
                  kv_prev, kv_prev, kv_prev, kv_prev],
        out_specs=pl.BlockSpec((tq, ATT_Q_W), row),
        out_shape=jax.ShapeDtypeStruct((t, ATT_Q_W), BF16),
        compiler_params=_cparams(("parallel",), 32),
        name="window_attention",
    )(sinks, aq, ak, akx, av, avx, ak, akx, av, avx)


def _gla_kernel(q_ref, k_ref, la_ref, v_ref, g_ref, nw_ref, o_ref, state_ref, *, nchunk):
    @pl.when(pl.program_id(1) == 0)
    def _():
        state_ref[...] = jnp.zeros_like(state_ref)

    ch = GLA_CHUNK
    tri = (lax.broadcasted_iota(jnp.int32, (ch, ch), 0) >= lax.broadcasted_iota(jnp.int32, (ch, ch), 1))
    tri_f = tri.astype(F32)
    ones = jnp.ones((ch, LANES), F32)
    lane = lax.broadcasted_iota(jnp.int32, (ch, LANES), 1)
    srow = lax.broadcasted_iota(jnp.int32, (LANES, GLA_DV), 0)
    scale = GLA_DK ** -0.5
    mid = ch // 2 - 1
    nw = nw_ref[...]
    tn = (((0,), (0,)), ((), ()))
    nt = (((1,), (1,)), ((), ()))
    for cidx in range(nchunk):
        rows = slice(cidx * ch, (cidx + 1) * ch)
        for pair in range(GLA_HEADS // 2):
            cols = slice(pair * LANES, (pair + 1) * LANES)
            g = la_ref[rows, cols]
            b = jnp.dot(tri_f, g, preferred_element_type=F32, precision=HIGHEST)
            b_last_col = lax.dot_general(g, ones, tn, preferred_element_type=F32, precision=HIGHEST)
            decay = jnp.exp(b_last_col)
            b_mid = b[mid:mid + 1, :]
            b_last = b[ch - 1:ch, :]
            qt = q_ref[rows, cols] * scale * jnp.exp(b - b_mid)
            kt = k_ref[rows, cols] * jnp.exp(b_mid - b)
            q_in = (qt * jnp.exp(b_mid)).astype(BF16)
            k_st = (kt * jnp.exp(b_last - b_mid)).astype(BF16)
            qt = qt.astype(BF16)
            for hh in range(2):
                h = 2 * pair + hh
                mine = (lane >= hh * GLA_DK) & (lane < (hh + 1) * GLA_DK)
                ktm = jnp.where(mine, kt, 0.0).astype(BF16)
                att = lax.dot_general(qt, ktm, nt, preferred_element_type=F32)
                att = jnp.where(tri, att, 0.0)
                vh = v_ref[rows, h * GLA_DV:(h + 1) * GLA_DV]
                st = state_ref[h]
                o = (jnp.dot(att.astype(BF16), vh, preferred_element_type=F32)
                     + jnp.dot(q_in, st.astype(BF16), preferred_element_type=F32))
                gate = _silu(g_ref[rows, h * GLA_DV:(h + 1) * GLA_DV])
                o_ref[rows, h * GLA_DV:(h + 1) * GLA_DV] = (_rms(o, nw) * gate).astype(o_ref.dtype)
                upd = lax.dot_general(k_st, vh, tn, preferred_element_type=F32)
                rmask = (srow >= hh * GLA_DK) & (srow < (hh + 1) * GLA_DK)
                state_ref[h] = jnp.where(rmask, decay * st + upd, 0.0)


def _gla(gq, gk, la, gv, gg, norm_w, bsz, seq, cb):
    t = gq.shape[0]
    steps = seq // cb
    row = lambda b, s: (b * steps + s, 0)
    return pl.pallas_call(
        functools.partial(_gla_kernel, nchunk=cb // GLA_CHUNK),
        grid=(bsz, steps),
        in_specs=[pl.BlockSpec((cb, GLA_QK_W), row), pl.BlockSpec((cb, GLA_QK_W), row),
                  pl.BlockSpec((cb, GLA_QK_W), row), pl.BlockSpec((cb, GLA_V_W), row),
                  pl.BlockSpec((cb, GLA_V_W), row), pl.BlockSpec((1, GLA_DV), lambda b, s: (0, 0))],
        out_specs=pl.BlockSpec((cb, GLA_V_W), row),
        out_shape=jax.ShapeDtypeStruct((t, GLA_V_W), BF16),
        scratch_shapes=[pltpu.VMEM((GLA_HEADS, LANES, GLA_DV), F32)],
        compiler_params=_cparams(("parallel", "arbitrary"), 32),
        name="gated_linear_attention",
    )(gq, gk, la, gv, gg, norm_w.reshape(1, GLA_DV))


def _mix_out_kernel(att_ref, go_ref, x_ref, mod_ref, wa_ref, wg_ref, nw_ref, wq_ref, x1_ref, h2_ref, qp_ref):
    mixed = (jnp.dot(att_ref[...], wa_ref[...], preferred_element_type=F32)
             + jnp.dot(go_ref[...], wg_ref[...], preferred_element_type=F32))
    x1 = x_ref[...] + mod_ref[0, 2:3, :] * mixed
    x1_ref[...] = x1
    h2 = (_rms(x1, nw_ref[...]) * (1.0 + mod_ref[0, 4:5, :]) + mod_ref[0, 3:4, :]).astype(BF16)
    h2_ref[...] = h2
    qp_ref[...] = jnp.dot(h2, wq_ref[...], preferred_element_type=F32).astype(BF16)


def _mix_out(att, go, x2, mod, w_out, norm_w, wq, seq, tb):
    t, d = x2.shape
    nq = wq.shape[1]
    steps_per_seq = seq // tb
    row = lambda i: (i, 0)
    fixed = lambda i: (0, 0)
    wa = w_out[:ATT_Q_W].astype(BF16)
    wg = w_out[ATT_Q_W:].astype(BF16)
    return pl.pallas_call(
        _mix_out_kernel,
        grid=(t // tb,),
        in_specs=[pl.BlockSpec((tb, ATT_Q_W), row), pl.BlockSpec((tb, GLA_V_W), row), pl.BlockSpec((tb, d), row),
                  pl.BlockSpec((1, SUBLANES, d), lambda i: (i // steps_per_seq, 0, 0)),
                  pl.BlockSpec((ATT_Q_W, d), fixed), pl.BlockSpec((GLA_V_W, d), fixed),
                  pl.BlockSpec((1, d), fixed), pl.BlockSpec((d, nq), fixed)],
        out_specs=[pl.BlockSpec((tb, d), row), pl.BlockSpec((tb, d), row), pl.BlockSpec((tb, nq), row)],
        out_shape=[jax.ShapeDtypeStruct((t, d), F32), jax.ShapeDtypeStruct((t, d), BF16),
                   jax.ShapeDtypeStruct((t, nq), BF16)],
        compiler_params=_cparams(("parallel",), 48),
        name="out_proj_norm2_peer_query",
    )(att, go, x2, mod, wa, wg, norm_w.reshape(1, d), wq.astype(BF16))


_NO_ID = 1e9


def _top16(x, ids):
    vals, idxs = [], []
    for _ in range(PEER_TOPK):
        m = jnp.max(x, axis=0, keepdims=True)
        idx = jnp.min(jnp.where(x == m, ids, _NO_ID), axis=0, keepdims=True)
        vals.append(m)
        idxs.append(idx)
        x = jnp.where(ids == idx, NEG_INF, x)
    return jnp.concatenate(vals, axis=0), jnp.concatenate(idxs, axis=0)


def _pick_rows(table, sel):
    out = jnp.zeros_like(table)
    for r in range(table.shape[0]):
        out = out + jnp.where(sel == float(r), table[r:r + 1, :], 0.0)
    return out


_PAIR_RUNS = tuple(PEER_TOPK // (r + 1) for r in range(PEER_TOPK))
_PAIR_ROWS = -(-sum(_PAIR_RUNS) // SUBLANES) * SUBLANES


def _topk_kernel(q_ref, sk_ref, i_ref, j_ref, g_ref, *, tb):
    q = q_ref[...]
    nt = (((1,), (1,)), ((), ()))
    scores = [lax.dot_general(sk_ref[0, p], q[:, p * N_KEYS:(p + 1) * N_KEYS], nt, preferred_element_type=F32)
              for p in range(2)]
    key_ids = lax.broadcasted_iota(jnp.int32, (N_KEYS, LANES), 0).astype(F32)
    run_ids = lax.broadcasted_iota(jnp.int32, (PEER_TOPK, LANES), 0).astype(F32)
    pad = _PAIR_ROWS - sum(_PAIR_RUNS)
    pair_ids = jnp.concatenate([run_ids[:n] + float(r * PEER_TOPK) for r, n in enumerate(_PAIR_RUNS)]
                               + [jnp.full((pad, LANES), _NO_ID, F32)], axis=0)
    for lt in range(tb // LANES):
        lanes = slice(lt * LANES, (lt + 1) * LANES)
        a, ia = _top16(scores[0][:, lanes], key_ids)
        b, ib = _top16(scores[1][:, lanes], key_ids)
        cand = jnp.concatenate([a[r:r + 1, :] + b[:n] for r, n in enumerate(_PAIR_RUNS)]
                               + [jnp.full((pad, LANES), NEG_INF, F32)], axis=0)
        best, flat = _top16(cand, pair_ids)
        pa = jnp.floor(flat * (1.0 / PEER_TOPK))
        pb = flat - pa * PEER_TOPK
        i_ref[:, lanes] = _pick_rows(ia, pa).astype(jnp.int32)
        j_ref[:, lanes] = _pick_rows(ib, pb).astype(jnp.int32)
        e = jnp.exp(best - best[0:1, :])
        g_ref[:, lanes] = e / jnp.sum(e, axis=0, keepdims=True)


def _peer_topk(qp, subkeys, tb):
    t = qp.shape[0]
    sk = subkeys.astype(BF16)
    blk = lambda i, h: (h, i)
    outs = pl.pallas_call(
        functools.partial(_topk_kernel, tb=tb),
        grid=(t // tb, PEER_HEADS),
        in_specs=[pl.BlockSpec((tb, PEER_QDIM), lambda i, h: (i, h)),
                  pl.BlockSpec((1, 2, N_KEYS, PEER_QDIM // 2), lambda i, h: (h, 0, 0, 0))],
        out_specs=[pl.BlockSpec((PEER_TOPK, tb), blk)] * 3,
        out_shape=[jax.ShapeDtypeStruct((PEER_SLOTS, t), jnp.int32), jax.ShapeDtypeStruct((PEER_SLOTS, t), jnp.int32),
                   jax.ShapeDtypeStruct((PEER_SLOTS, t), F32)],
        compiler_params=_cparams(("parallel", "parallel"), 32),
        name="peer_product_key_topk",
    )(qp, sk)
    return tuple(o.T for o in outs)


def _gelu(x):
    return 0.5 * x * (1.0 + lax.erf(x * (2.0 ** -0.5)))


def _peer_score_kernel(h_ref, ut_ref, i_ref, j_ref, g_ref, c_ref, *, groups):
    h = h_ref[...]
    ii = i_ref[...]
    jj = j_ref[...]

    def tile(e, acc):
        s = jnp.dot(h, ut_ref[e], preferred_element_type=F32)
        for g in range(groups):
            picked = jnp.take_along_axis(s[:, g * N_KEYS:(g + 1) * N_KEYS], jj, axis=1)
            acc = acc + jnp.where(ii == e * groups + g, picked, 0.0)
        return acc

    acc = lax.fori_loop(0, ut_ref.shape[0], tile, jnp.zeros(ii.shape, F32))
    c_ref[...] = _gelu(acc) * g_ref[...]


def _peer_scores(h2, u, ii, jj, gates, tb, te):
    t, d = h2.shape
    ne = u.shape[0]
    ut = u.reshape(ne // te, te, d).transpose(0, 2, 1).astype(BF16)
    row = lambda i: (i, 0)
    slot_spec = pl.BlockSpec((tb, PEER_SLOTS), row)
    return pl.pallas_call(
        functools.partial(_peer_score_kernel, groups=te // N_KEYS),
        grid=(t // tb,),
        in_specs=[pl.BlockSpec((tb, d), row),
                  pl.BlockSpec((ne // te, d, te), lambda i: (0, 0, 0), pipeline_mode=pl.Buffered(1)),
                  slot_spec, slot_spec, slot_spec],
        out_specs=slot_spec,
        out_shape=jax.ShapeDtypeStruct((t, PEER_SLOTS), F32),
        compiler_params=_cparams(("arbitrary",), 56),
        name="peer_expert_scores",
    )(h2, ut, ii, jj, gates)


def _peer_mix_kernel(c_ref, i_ref, j_ref, v_ref, x1_ref, mod_ref, fw_ref, o_ref, coef_ref, lhs_ref, *, tb, sb):
    sub = lax.broadcasted_iota(jnp.int32, (N_KEYS, PEER_SLOTS), 0)
    nt = (((1,), (1,)), ((), ()))

    def sub_block(s, carry):
        base = pl.multiple_of(s * sb, sb)

        def one_token(t, carry):
            irow = i_ref[pl.ds(base + t, 1), :]
            jrow = j_ref[pl.ds(base + t, 1), :]
            crow = c_ref[pl.ds(base + t, 1), :]
            a_t = jnp.where(sub == irow, crow, 0.0).astype(BF16)
            b_t = jnp.where(sub == jrow, 1.0, 0.0).astype(BF16)
            tile = lax.dot_general(a_t, b_t, nt, preferred_element_type=F32)
            coef_ref[pl.ds(pl.multiple_of(t * COEF_PITCH, SUBLANES), N_KEYS), :] = tile
            return carry

        lax.fori_loop(0, sb, one_token, 0, unroll=32)
        for g in range(N_KEYS):
            lhs_ref[pl.ds(base, sb), g * N_KEYS:(g + 1) * N_KEYS] = (
                coef_ref[pl.ds(g, sb, stride=COEF_PITCH), :].astype(BF16))
        return carry

    lax.fori_loop(0, tb // sb, sub_block, 0)
    mixed = jnp.dot(lhs_ref[...], v_ref[...], preferred_element_type=F32)
    y = x1_ref[...] + mod_ref[0, 5:6, :] * mixed
    o_ref[...] = _rms(y, fw_ref[...])


def _peer_mix(c, ii, jj, v_bf, x1, mod, final_w, seq, tb, sb):
    t, d = x1.shape
    ne = v_bf.shape[0]
    steps_per_seq = seq // tb
    row = lambda i: (i, 0)
    slot_spec = pl.BlockSpec((tb, PEER_SLOTS), row)
    return pl.pallas_call(
        functools.partial(_peer_mix_kernel, tb=tb, sb=sb),
        grid=(t // tb,),
        in_specs=[slot_spec, slot_spec, slot_spec,
                  pl.BlockSpec((ne, d), lambda i: (0, 0), pipeline_mode=pl.Buffered(1)),
                  pl.BlockSpec((tb, d), row),
                  pl.BlockSpec((1, SUBLANES, d), lambda i: (i // steps_per_seq, 0, 0)),
                  pl.BlockSpec((1, d), lambda i: (0, 0))],
        out_specs=pl.BlockSpec((tb, d), row),
        out_shape=jax.ShapeDtypeStruct((t, d), F32),
        scratch_shapes=[pltpu.VMEM((sb * COEF_PITCH, N_KEYS), F32), pltpu.VMEM((tb, ne), BF16)],
        compiler_params=_cparams(("arbitrary",), 60),
        name="peer_expert_mix_final_norm",
    )(c, ii, jj, v_bf, x1, mod, final_w.reshape(1, d))


def _pick_block(n, want):
    b = min(n, want)
    assert n % b == 0, (n, want)
    return b


def kernel(x, c, w_ada, b_ada, norm1_w, w_in, attn_sinks, gla_gate_up, gla_gate_bias, gla_norm_w, w_out, norm2_w,
           peer_wq, peer_subkeys, peer_u, peer_v, final_norm_w):
    bsz, seq, d = x.shape
    assert w_ada.shape[0] == 1, "the block is implemented for depth 1"
    assert seq % WINDOW == 0 and seq % GLA_CHUNK == 0
    assert peer_u.shape[1] == N_KEYS * N_KEYS
    tok_blk = _pick_block(seq, 512)
    x2 = x.reshape(bsz * seq, d)
    mod = _modulation(c, w_ada[0], b_ada[0])
    aq, ak, akx, av, avx, gq, gk, gv, gg, la = _in_proj(
        x2, mod, norm1_w[0], w_in[0], gla_gate_up[0], gla_gate_bias[0], seq, tok_blk)
    att = _attention(aq, ak, akx, av, avx, attn_sinks[0], seq, tok_blk)
    go = _gla(gq, gk, la, gv, gg, gla_norm_w[0], bsz, seq, tok_blk)
    x1, h2, qp = _mix_out(att, go, x2, mod, w_out[0], norm2_w[0], peer_wq[0], seq, tok_blk)
    ii, jj, gates = _peer_topk(qp, peer_subkeys[0], tok_blk)
    coef = _peer_scores(h2, peer_u[0], ii, jj, gates, tok_blk, 2048)
    out = _peer_mix(coef, ii, jj, peer_v[0].astype(BF16), x1, mod, final_norm_w, seq, _pick_block(seq, 256), 64)
    return out.reshape(bsz, seq, d)
```
